```python
import jax, jax.numpy as jnp
from jax import lax
import numpy as np

D_MODEL = 1024
BATCH = 8
SEQ = 4096
DEPTH = 1

SWA_HEADS = 8
SWA_HEAD_DIM = 64
DILATED_PATTERNS = ((128, 1), (512, 4), (2048, 16))
MLA_HEADS = 8
MLA_NOPE_DIM = 64
MLA_ROPE_DIM = 32
MLA_V_DIM = 64
MLA_Q_RANK = 256
MLA_KV_RANK = 256
ROPE_THETA = 10000.0
Q_BLOCK = 128
SWA_WIDTH = SWA_HEADS * SWA_HEAD_DIM
MLA_WIDTH = MLA_HEADS * MLA_V_DIM
MIX_WIDTH = SWA_WIDTH + MLA_WIDTH
IN_SPLITS = (SWA_WIDTH, 2 * SWA_WIDTH, 3 * SWA_WIDTH, 3 * SWA_WIDTH + MLA_Q_RANK,
             3 * SWA_WIDTH + MLA_Q_RANK + MLA_KV_RANK)
IN_PROJ_WIDTH = 3 * SWA_WIDTH + MLA_Q_RANK + MLA_KV_RANK + MLA_ROPE_DIM
PEER_HEADS = 8
PEER_N_KEYS = 128
PEER_N_EXPERTS = PEER_N_KEYS * PEER_N_KEYS
PEER_TOPK = 16
PEER_QUERY_DIM = 256
PEER_HALF_DIM = PEER_QUERY_DIM // 2
PEER_TOKEN_CHUNK = 128
NORM_EPS = 1e-6
NEG_INF = -1e30

kernel_name = 'hybrid_dilated_mla_peer_block'


def rms_norm(x, g):
    xf = x.astype(jnp.float32)
    y = xf * lax.rsqrt(jnp.mean(xf * xf, axis=-1, keepdims=True) + NORM_EPS)
    return (y * g.astype(jnp.float32)).astype(x.dtype)


def rope_tables(seq, dim):
    inv = 1.0 / (ROPE_THETA ** (jnp.arange(0, dim, 2, dtype=jnp.float32) / dim))
    ang = jnp.arange(seq, dtype=jnp.float32)[:, None] * inv[None, :]
    return jnp.cos(ang), jnp.sin(ang)


def apply_rope(x, cos, sin):
    half = x.shape[-1] // 2
    x1 = x[..., :half].astype(jnp.float32)
    x2 = x[..., half:].astype(jnp.float32)
    return jnp.concatenate([x1 * cos - x2 * sin, x2 * cos + x1 * sin], axis=-1).astype(x.dtype)


def dilated_branch(q, k, v, window, dilation):
    b, h, s, hd = q.shape
    span = window // dilation
    unit = span * dilation
    s_pad = -(-s // unit) * unit
    m = s_pad // dilation
    nb = m // span
    pad = ((0, 0), (0, 0), (0, s_pad - s), (0, 0))

    def to_blocks(t):
        t = jnp.pad(t, pad).reshape(b, h, m, dilation, hd).transpose(0, 1, 3, 2, 4)
        return t.reshape(b, h, dilation, nb, span, hd)

    def with_prev(t):
        prev = jnp.pad(t[:, :, :, :-1], ((0, 0), (0, 0), (0, 0), (1, 0), (0, 0), (0, 0)))
        return jnp.concatenate([prev, t], axis=4)

    qb = to_blocks(q)
    kk = with_prev(to_blocks(k))
    vv = with_prev(to_blocks(v))
    i = jnp.arange(span)[:, None]
    c = jnp.arange(2 * span)[None, :]
    rel = span + i - c
    band = (rel >= 0) & (rel <= span)
    blk = jnp.arange(nb)[:, None, None]
    mask = band[None] & ((blk > 0) | (c >= span)[None])
    sc = jnp.einsum('bhrnqd,bhrnkd->bhrnqk', qb, kk, preferred_element_type=jnp.float32) * (hd ** -0.5)
    sc = jnp.where(mask, sc, NEG_INF)
    lse = jax.nn.logsumexp(sc, axis=-1)
    p = jnp.exp(sc - lse[..., None]).astype(v.dtype)
    o = jnp.einsum('bhrnqk,bhrnkd->bhrnqd', p, vv)
    o = o.reshape(b, h, dilation, m, hd).transpose(0, 1, 3, 2, 4).reshape(b, h, s_pad, hd)[:, :, :s]
    lse = lse.reshape(b, h, dilation, m).transpose(0, 1, 3, 2).reshape(b, h, s_pad)[:, :, :s]
    return o, lse


def dilated_attention(q, k, v):
    outs, lses = [], []
    for window, dilation in DILATED_PATTERNS:
        o, l = dilated_branch(q, k, v, window, dilation)
        outs.append(o)
        lses.append(l)
    wts = jax.nn.softmax(jnp.stack(lses, axis=0), axis=0)
    out = jnp.sum(wts[..., None] * jnp.stack(outs, axis=0).astype(jnp.float32), axis=0)
    return out.astype(q.dtype)


def mla_attention(c_q, c_kv, k_rope, q_norm_g, kv_norm_g, w_uq, w_uk, w_uv, cos, sin):
    b, s, _ = c_q.shape
    q = jnp.einsum('bsr,rf->bsf', rms_norm(c_q, q_norm_g), w_uq)
    q = q.reshape(b, s, MLA_HEADS, MLA_NOPE_DIM + MLA_ROPE_DIM).transpose(0, 2, 1, 3)
    q_nope = q[..., :MLA_NOPE_DIM]
    q_rope = apply_rope(q[..., MLA_NOPE_DIM:], cos, sin)
    ckv = rms_norm(c_kv, kv_norm_g)
    k_nope = jnp.einsum('bsr,rf->bsf', ckv, w_uk).reshape(b, s, MLA_HEADS, MLA_NOPE_DIM).transpose(0, 2, 1, 3)
    v = jnp.einsum('bsr,rf->bsf', ckv, w_uv).reshape(b, s, MLA_HEADS, MLA_V_DIM).transpose(0, 2, 1, 3)
    k_r = apply_rope(k_rope, cos, sin)
    scale = (MLA_NOPE_DIM + MLA_ROPE_DIM) ** -0.5
    nb = s // Q_BLOCK
    qn_b = q_nope.reshape(b, MLA_HEADS, nb, Q_BLOCK, MLA_NOPE_DIM).transpose(2, 0, 1, 3, 4)
    qr_b = q_rope.reshape(b, MLA_HEADS, nb, Q_BLOCK, MLA_ROPE_DIM).transpose(2, 0, 1, 3, 4)
    kpos = jnp.arange(s)

    def block(args):
        qn, qr, start = args
        sc = (jnp.einsum('bhqd,bhkd->bhqk', qn, k_nope, preferred_element_type=jnp.float32)
              + jnp.einsum('bhqd,bkd->bhqk', qr, k_r, preferred_element_type=jnp.float32)) * scale
        qpos = start + jnp.arange(Q_BLOCK)
        sc = jnp.where(kpos[None, :] <= qpos[:, None], sc, NEG_INF)
        p = jax.nn.softmax(sc, axis=-1).astype(v.dtype)
        return jnp.einsum('bhqk,bhkd->bhqd', p, v)

    o = lax.map(block, (qn_b, qr_b, jnp.arange(nb) * Q_BLOCK))
    return o.transpose(1, 0, 3, 2, 4).reshape(b, s, MLA_HEADS * MLA_V_DIM)


def peer_ffn(x, w_query, sub_keys, expert_u, expert_v):
    b, s, d = x.shape
    n_tok = b * s
    n_chunks = n_tok // PEER_TOKEN_CHUNK

    def chunk(tc):
        q = (tc @ w_query).reshape(PEER_TOKEN_CHUNK, PEER_HEADS, 2, PEER_HALF_DIM)
        sc = jnp.einsum('thpc,hpnc->thpn', q, sub_keys, preferred_element_type=jnp.float32)
        s_top, i_top = lax.top_k(sc, PEER_TOPK)
        cand = s_top[:, :, 0, :, None] + s_top[:, :, 1, None, :]
        cand_idx = i_top[:, :, 0, :, None] * PEER_N_KEYS + i_top[:, :, 1, None, :]
        cand = cand.reshape(PEER_TOKEN_CHUNK, PEER_HEADS, PEER_TOPK * PEER_TOPK)
        cand_idx = cand_idx.reshape(PEER_TOKEN_CHUNK, PEER_HEADS, PEER_TOPK * PEER_TOPK)
        best, pos = lax.top_k(cand, PEER_TOPK)
        idx = jnp.take_along_axis(cand_idx, pos, axis=-1)
        gate = jax.nn.softmax(best, axis=-1)
        u = expert_u[idx]
        hid = jax.nn.gelu(jnp.einsum('cd,chkd->chk', tc, u, preferred_element_type=jnp.float32),
                          approximate=False) * gate
        vv = expert_v[idx]
        return jnp.einsum('chk,chkd->cd', hid.astype(vv.dtype), vv)

    out = lax.map(chunk, x.reshape(n_chunks, PEER_TOKEN_CHUNK, d))
    return out.reshape(b, s, d)


def setup_inputs(seed: int = 0) -> dict:
    key = jax.random.key(seed)
    ks = jax.random.split(key, 16)
    L = DEPTH

    def w(k, shape, fan_in):
        return jax.random.normal(k, shape, jnp.float32) * (fan_in ** -0.5)

    def gain(k, shape):
        return 1.0 + 0.01 * jax.random.normal(k, shape, jnp.float32)

    return {
        'x': jax.random.normal(ks[0], (BATCH, SEQ, D_MODEL), jnp.float32),
        'attn_norm_g': gain(ks[1], (L, D_MODEL)),
        'w_in': w(ks[2], (L, D_MODEL, IN_PROJ_WIDTH), D_MODEL),
        'mla_q_norm_g': gain(ks[3], (L, MLA_Q_RANK)),
        'mla_kv_norm_g': gain(ks[4], (L, MLA_KV_RANK)),
        'w_uq': w(ks[5], (L, MLA_Q_RANK, MLA_HEADS * (MLA_NOPE_DIM + MLA_ROPE_DIM)), MLA_Q_RANK),
        'w_uk': w(ks[6], (L, MLA_KV_RANK, MLA_HEADS * MLA_NOPE_DIM), MLA_KV_RANK),
        'w_uv': w(ks[7], (L, MLA_KV_RANK, MLA_HEADS * MLA_V_DIM), MLA_KV_RANK),
        'w_o': w(ks[8], (L, MIX_WIDTH, D_MODEL), MIX_WIDTH),
        'ffn_norm_g': gain(ks[9], (L, D_MODEL)),
        'peer_w_query': w(ks[10], (L, D_MODEL, PEER_HEADS * PEER_QUERY_DIM), D_MODEL),
        'peer_sub_keys': w(ks[11], (L, PEER_HEADS, 2, PEER_N_KEYS, PEER_HALF_DIM), PEER_HALF_DIM),
        'peer_u': w(ks[12], (L, PEER_N_EXPERTS, D_MODEL), D_MODEL),
        'peer_v': w(ks[13], (L, PEER_N_EXPERTS, D_MODEL), PEER_TOPK),
        'final_norm_g': gain(ks[14], (D_MODEL,)),
    }


def reference(x, attn_norm_g, w_in, mla_q_norm_g, mla_kv_norm_g, w_uq, w_uk, w_uv, w_o,
              ffn_norm_g, peer_w_query, peer_sub_keys, peer_u, peer_v, final_norm_g):
    b, s, _ = x.shape
    cos_a, sin_a = rope_tables(s, SWA_HEAD_DIM)
    cos_b, sin_b = rope_tables(s, MLA_ROPE_DIM)
    for layer in range(DEPTH):
        h = rms_norm(x, attn_norm_g[layer])
        proj = jnp.einsum('bsd,df->bsf', h, w_in[layer])
        q_a, k_a, v_a, c_q, c_kv, k_rope = jnp.split(proj, IN_SPLITS, axis=-1)

        def heads(t):
            return t.reshape(b, s, SWA_HEADS, SWA_HEAD_DIM).transpose(0, 2, 1, 3)

        q_a = apply_rope(heads(q_a), cos_a, sin_a)
        k_a = apply_rope(heads(k_a), cos_a, sin_a)
        o_a = dilated_attention(q_a, k_a, heads(v_a)).transpose(0, 2, 1, 3).reshape(b, s, SWA_WIDTH)
        o_b = mla_attention(c_q, c_kv, k_rope, mla_q_norm_g[layer], mla_kv_norm_g[layer],
                            w_uq[layer], w_uk[layer], w_uv[layer], cos_b, sin_b)
        mixed = jnp.concatenate([o_a, o_b], axis=-1)
        x = x + jnp.einsum('bsf,fd->bsd', mixed, w_o[layer])
        h = rms_norm(x, ffn_norm_g[layer])
        x = x + peer_ffn(h, peer_w_query[layer], peer_sub_keys[layer], peer_u[layer], peer_v[layer])
    return rms_norm(x, final_norm_g)
```

```python
import functools

import jax
import jax.numpy as jnp
import numpy as np
from jax import lax
from jax.experimental import pallas as pl
from jax.experimental.pallas import tpu as pltpu

D_MODEL = 1024
SWA_HEADS = 8
SWA_HEAD_DIM = 64
DILATIONS = (1, 4, 16)
SPAN = 128
MLA_HEADS = 8
MLA_NOPE_DIM = 64
MLA_ROPE_DIM = 32
MLA_V_DIM = 64
MLA_Q_RANK = 256
MLA_KV_RANK = 256
ROPE_THETA = 10000.0
SWA_WIDTH = SWA_HEADS * SWA_HEAD_DIM
PEER_HEADS = 8
PEER_N_KEYS = 128
PEER_TOPK = 16
PEER_HALF_DIM = 128
NORM_EPS = 1e-6
NEG_INF = -1e30

LANES = 128
N_HEADS = 8
SLAB_W = N_HEADS * LANES
VMEM_LIMIT = 56 * 1024 * 1024

F32 = jnp.float32
BF16 = jnp.bfloat16


def _rms(x, g):
    return x * lax.rsqrt(jnp.mean(x * x, axis=-1, keepdims=True) + NORM_EPS) * g


def _dot(a, b):
    return jnp.dot(a, b, preferred_element_type=F32)


def _dot_nt(a, b):
    return lax.dot_general(a, b, (((1,), (1,)), ((), ())), preferred_element_type=F32)


_C_QA, _C_QAR, _C_KA, _C_KAR, _C_VA = 0, SLAB_W, 2 * SLAB_W, 3 * SLAB_W, 4 * SLAB_W
_C_CQ = 5 * SLAB_W
_C_CKV = _C_CQ + MLA_Q_RANK
_C_KR = _C_CKV + MLA_KV_RANK
_C_END = _C_KR + LANES


def _proj_kernel(x_ref, g_ref, w1_ref, gq_ref, gkv_ref, wq_ref, wkv_ref, place_ref,
                 ta_c_ref, ta_s_ref, tq_c_ref, tq_s_ref, tk_ref,
                 qa_ref, ka_ref, va_ref, qb_ref, kb_ref, vb_ref):
    hn = _rms(x_ref[0], g_ref[...]).astype(BF16)
    half = SLAB_W // 2

    def mm(lo, width):
        return _dot(hn, w1_ref[:, lo:lo + width])

    def store_heads(out_ref, h0, val):
        for j in range(val.shape[1] // LANES):
            out_ref[h0 + j, 0] = val[:, j * LANES:(j + 1) * LANES].astype(BF16)

    def rope_heads(out_ref, h0, plain, rot, c, s):
        for j in range(plain.shape[1] // LANES):
            sl = slice(j * LANES, (j + 1) * LANES)
            out_ref[h0 + j, 0] = (plain[:, sl] * c + rot[:, sl] * s).astype(BF16)

    ca, sa = ta_c_ref[...], ta_s_ref[...]
    for part in range(2):
        off = part * half
        rope_heads(qa_ref, part * 4, mm(_C_QA + off, half), mm(_C_QAR + off, half), ca, sa)
        rope_heads(ka_ref, part * 4, mm(_C_KA + off, half), mm(_C_KAR + off, half), ca, sa)
        store_heads(va_ref, part * 4, mm(_C_VA + off, half))

    cqn = _rms(mm(_C_CQ, MLA_Q_RANK), gq_ref[...]).astype(BF16)
    cq_c, cq_s = tq_c_ref[...], tq_s_ref[...]
    for part in range(2):
        off = part * half
        rope_heads(qb_ref, part * 4, _dot(cqn, wq_ref[:, off:off + half]),
                   _dot(cqn, wq_ref[:, SLAB_W + off:SLAB_W + off + half]), cq_c, cq_s)

    ckvn = _rms(mm(_C_CKV, MLA_KV_RANK), gkv_ref[...]).astype(BF16)
    kr = (mm(_C_KR, LANES) * tk_ref[...]).astype(BF16)
    for part in range(2):
        off = part * half
        store_heads(kb_ref, part * 4,
                    _dot(ckvn, wkv_ref[:, off:off + half]) + _dot(kr, place_ref[:, off:off + half]))
        store_heads(vb_ref, part * 4, _dot(ckvn, wkv_ref[:, SLAB_W + off:SLAB_W + off + half]))


def _proj_call(x, g, w1, gq, gkv, wq, wkv, place, tabs, tm):
    b, s, d = x.shape
    const = lambda shape: pl.BlockSpec(shape, lambda bi, i: (0,) * len(shape))
    tab = pl.BlockSpec((tm, LANES), lambda bi, i: (i, 0))
    head_out = pl.BlockSpec((N_HEADS, 1, tm, LANES), lambda bi, i: (0, bi, i, 0))
    out_sds = jax.ShapeDtypeStruct((N_HEADS, b, s, LANES), BF16)
    return pl.pallas_call(
        _proj_kernel,
        grid=(b, s // tm),
        in_specs=[pl.BlockSpec((1, tm, d), lambda bi, i: (bi, i, 0)),
                  const(g.shape), const(w1.shape), const(gq.shape), const(gkv.shape),
                  const(wq.shape), const(wkv.shape), const(place.shape),
                  tab, tab, tab, tab, tab],
        out_specs=[head_out] * 6,
        out_shape=[out_sds] * 6,
        compiler_params=pltpu.CompilerParams(
            dimension_semantics=("arbitrary", "arbitrary"), vmem_limit_bytes=VMEM_LIMIT),
        name="proj",
    )(x, g, w1, gq, gkv, wq, wkv, place, *tabs)


def _dilated_kernel(q1, k1, v1, q4, k4, v4, q16, k16, v16, o_ref, m_s, l_s, acc_s):
    views = {1: (q1, k1, v1), 4: (q4, k4, v4), 16: (q16, k16, v16)}
    seq = o_ref.shape[2]
    row = lax.broadcasted_iota(jnp.int32, (SPAN, 2 * SPAN), 0)
    col = lax.broadcasted_iota(jnp.int32, (SPAN, 2 * SPAN), 1)
    band_bias = jnp.where(col >= row, jnp.where(col <= row + SPAN, 0.0, NEG_INF), NEG_INF)
    row0 = lax.broadcasted_iota(jnp.int32, (SPAN, SPAN), 0)
    col0 = lax.broadcasted_iota(jnp.int32, (SPAN, SPAN), 1)
    first_bias = jnp.where(col0 <= row0, 0.0, NEG_INF)

    def softmax_block(s, v):
        m = jnp.max(s, axis=1, keepdims=True)
        p = jnp.exp(s - m)
        l = jnp.sum(p, axis=1, keepdims=True)
        return m, l, _dot(p.astype(BF16), v)

    def merge(d, r, n, m, l, pv):
        m = jnp.broadcast_to(m, (SPAN, LANES))
        l = jnp.broadcast_to(l, (SPAN, LANES))
        if d == 1:
            rows = pl.ds(pl.multiple_of(n * SPAN, SPAN), SPAN)
            m_s[rows, :] = m
            l_s[rows, :] = l
            acc_s[rows, :] = pv
        else:
            rows = pl.ds(n * SPAN * d + r, SPAN, stride=d)
            m_old = m_s[rows, :]
            m_new = jnp.maximum(m_old, m)
            a = jnp.exp(m_old - m_new)
            c = jnp.exp(m - m_new)
            m_s[rows, :] = m_new
            l_s[rows, :] = l_s[rows, :] * a + l * c
            acc_s[rows, :] = acc_s[rows, :] * a + pv * c

    for d in DILATIONS:
        q_ref, k_ref, v_ref = views[d]
        n_blocks = seq // d // SPAN
        for r in range(d):
            lanes = slice(r * LANES, (r + 1) * LANES)

            def rows_of(ref, n, lanes=lanes):
                return ref[0, 0, pl.ds(pl.multiple_of(n * SPAN, SPAN), SPAN), lanes]

            s0 = _dot_nt(rows_of(q_ref, 0), rows_of(k_ref, 0)) + first_bias
            merge(d, r, 0, *softmax_block(s0, rows_of(v_ref, 0)))

            def body(n, carry, d=d, r=r, q_ref=q_ref, k_ref=k_ref, v_ref=v_ref, rows_of=rows_of):
                k = jnp.concatenate([rows_of(k_ref, n - 1), rows_of(k_ref, n)], axis=0)
                v = jnp.concatenate([rows_of(v_ref, n - 1), rows_of(v_ref, n)], axis=0)
                s = _dot_nt(rows_of(q_ref, n), k) + band_bias
                merge(d, r, n, *softmax_block(s, v))
                return carry

            lax.fori_loop(1, n_blocks, body, 0)

    o_ref[0, 0] = (acc_s[...] / l_s[...]).astype(BF16)


def _dilated_call(qa, ka, va):
    h, b, s, _ = qa.shape
    operands, specs = [], []
    for d in DILATIONS:
        for t in (qa, ka, va):
            operands.append(t.reshape(h, b, s // d, d * LANES))
            specs.append(pl.BlockSpec((1, 1, s // d, d * LANES), lambda hi, bi: (hi, bi, 0, 0)))
    return pl.pallas_call(
        _dilated_kernel,
        grid=(h, b),
        in_specs=specs,
        out_specs=pl.BlockSpec((1, 1, s, LANES), lambda hi, bi: (hi, bi, 0, 0)),
        out_shape=jax.ShapeDtypeStruct((h, b, s, LANES), BF16),
        scratch_shapes=[pltpu.VMEM((s, LANES), F32)] * 3,
        compiler_params=pltpu.CompilerParams(
            dimension_semantics=("arbitrary", "arbitrary"), vmem_limit_bytes=VMEM_LIMIT),
        name="dilated",
    )(*operands)


def _mla_kernel(q_ref, k_ref, v_ref, o_ref, *, tq, tk):
    qi = pl.program_id(2)
    q = q_ref[0, 0]
    row = lax.broadcasted_iota(jnp.int32, (tq, tk), 0)
    col = lax.broadcasted_iota(jnp.int32, (tq, tk), 1)

    def step(j, carry, masked):
        m, l, acc = carry
        rows = pl.ds(pl.multiple_of(j * tk, tk), tk)
        s = _dot_nt(q, k_ref[0, 0, rows, :])
        if masked:
            s = jnp.where(col + j * tk <= row + qi * tq, s, NEG_INF)
        m_new = jnp.maximum(m, jnp.max(s, axis=1, keepdims=True))
        a = jnp.exp(m - m_new)
        p = jnp.exp(s - m_new)
        l = l * a + jnp.sum(p, axis=1, keepdims=True)
        acc = acc * a + _dot(p.astype(BF16), v_ref[0, 0, rows, :])
        return m_new, l, acc

    init = (jnp.full((tq, 1), NEG_INF, F32), jnp.zeros((tq, 1), F32), jnp.zeros((tq, LANES), F32))
    n_full = (qi * tq) // tk
    n_all = ((qi + 1) * tq + tk - 1) // tk
    carry = lax.fori_loop(0, n_full, functools.partial(step, masked=False), init)
    m, l, acc = lax.fori_loop(n_full, n_all, functools.partial(step, masked=True), carry)
    o_ref[0, 0] = (acc / l).astype(BF16)


def _mla_call(qb, kb, vb, tq, tk):
    h, b, s, _ = qb.shape
    kv_spec = pl.BlockSpec((1, 1, s, LANES), lambda hi, bi, qi: (hi, bi, 0, 0))
    q_spec = pl.BlockSpec((1, 1, tq, LANES), lambda hi, bi, qi: (hi, bi, qi, 0))
    return pl.pallas_call(
        functools.partial(_mla_kernel, tq=tq, tk=tk),
        grid=(h, b, s // tq),
        in_specs=[q_spec, kv_spec, kv_spec],
        out_specs=q_spec,
        out_shape=jax.ShapeDtypeStruct((h, b, s, LANES), BF16),
        compiler_params=pltpu.CompilerParams(
            dimension_semantics=("arbitrary", "arbitrary", "arbitrary"), vmem_limit_bytes=VMEM_LIMIT),
        name="mla",
    )(qb, kb, vb)


def _top_values(x, k):
    vals = []
    for _ in range(k):
        m = jnp.max(x, axis=0, keepdims=True)
        vals.append(m)
        x = jnp.where(x == m, NEG_INF, x)
    return jnp.concatenate(vals, axis=0)


def _kth_largest(x, k):
    for _ in range(k - 1):
        m = jnp.max(x, axis=0, keepdims=True)
        x = jnp.where(x == m, NEG_INF, x)
    return jnp.max(x, axis=0, keepdims=True)


def _mix_kernel(x_ref, oa_ref, ob_ref, wo_ref, g_ref, wq_ref, keys_ref,
                x1_ref, hn_ref, s1_ref, e1_ref, s2_ref, e2_ref):
    mixed = jnp.concatenate([oa_ref[h, 0] for h in range(N_HEADS)]
                            + [ob_ref[h, 0] for h in range(N_HEADS)], axis=1)
    x1 = x_ref[0] + _dot(mixed, wo_ref[...])
    x1_ref[0] = x1
    hn = _rms(x1, g_ref[...]).astype(BF16)
    hn_ref[0] = hn
    query = _dot(hn, wq_ref[...]).astype(BF16)
    for h in range(PEER_HEADS):
        lo = 2 * h * PEER_HALF_DIM
        s1 = _dot_nt(keys_ref[2 * h], query[:, lo:lo + PEER_HALF_DIM])
        s2 = _dot_nt(keys_ref[2 * h + 1], query[:, lo + PEER_HALF_DIM:lo + 2 * PEER_HALF_DIM])
        top1 = _top_values(s1, PEER_TOPK)
        top2 = _top_values(s2, PEER_TOPK)
        cand = jnp.concatenate([top1[i:i + 1] + top2 for i in range(PEER_TOPK)], axis=0)
        tau = _kth_largest(cand, PEER_TOPK)
        best = top1[0:1] + top2[0:1]
        z = jnp.sum(jnp.where(cand >= tau, jnp.exp(cand - best), 0.0), axis=0, keepdims=True)
        s1_ref[h] = s1 - tau
        s2_ref[h] = s2
        e1_ref[h] = jnp.exp(s1 - top1[0:1]) / z
        e2_ref[h] = jnp.exp(s2 - top2[0:1])


def _mix_call(x, oa, ob, wo, g, wq, keys, tm):
    b, s, d = x.shape
    const = lambda shape: pl.BlockSpec(shape, lambda bi, i: (0,) * len(shape))
    head_in = pl.BlockSpec((N_HEADS, 1, tm, LANES), lambda bi, i: (0, bi, i, 0))
    tok = pl.BlockSpec((1, tm, d), lambda bi, i: (bi, i, 0))
    tiles = s // tm
    score = pl.BlockSpec((PEER_HEADS, PEER_N_KEYS, tm), lambda bi, i: (0, 0, bi * tiles + i))
    score_sds = jax.ShapeDtypeStruct((PEER_HEADS, PEER_N_KEYS, b * s), F32)
    return pl.pallas_call(
        _mix_kernel,
        grid=(b, tiles),
        in_specs=[tok, head_in, head_in, const(wo.shape), const(g.shape), const(wq.shape),
                  const(keys.shape)],
        out_specs=[tok, tok, score, score, score, score],
        out_shape=[jax.ShapeDtypeStruct((b, s, d), F32), jax.ShapeDtypeStruct((b, s, d), BF16),
                   score_sds, score_sds, score_sds, score_sds],
        compiler_params=pltpu.CompilerParams(
            dimension_semantics=("arbitrary", "arbitrary"), vmem_limit_bytes=VMEM_LIMIT),
        name="mix",
    )(x, oa, ob, wo, g, wq, keys)


def _gelu(x):
    return 0.5 * x * (1.0 + lax.erf(x * 0.7071067811865476))


def _peer_kernel(hn_ref, x1_ref, u_ref, vt_ref, s1_ref, e1_ref, s2_ref, e2_ref, g_ref,
                 out_ref, acc_s, hid_s, act_s, *, rows_per_chunk):
    c = pl.program_id(1)
    tt = hn_ref.shape[0]

    @pl.when(c == 0)
    def _():
        acc_s[...] = jnp.zeros_like(acc_s)

    hid_s[...] = _dot_nt(u_ref[...], hn_ref[...])

    def lane_body(t, carry):
        lanes = pl.ds(pl.multiple_of(t * LANES, LANES), LANES)
        for i in range(rows_per_chunk):
            rows = slice(i * PEER_N_KEYS, (i + 1) * PEER_N_KEYS)
            gate = jnp.zeros((PEER_N_KEYS, LANES), F32)
            for h in range(PEER_HEADS):
                total = s2_ref[h, :, lanes] + s1_ref[h, i:i + 1, lanes]
                weight = e2_ref[h, :, lanes] * e1_ref[h, i:i + 1, lanes]
                gate = gate + jnp.where(total >= 0.0, weight, 0.0)
            act_s[rows, lanes] = (_gelu(hid_s[rows, lanes]) * gate).astype(BF16)
        return carry

    lax.fori_loop(0, tt // LANES, lane_body, 0)
    acc_s[...] += _dot(vt_ref[...], act_s[...])

    @pl.when(c == pl.num_programs(1) - 1)
    def _():
        y = x1_ref[...] + acc_s[...].T
        out_ref[...] = _rms(y, g_ref[...])


def _peer_call(hn, x1, u, vt, s1, e1, s2, e2, g, tt, rows_per_chunk):
    n, d = hn.shape
    ec = rows_per_chunk * PEER_N_KEYS
    n_chunks = u.shape[0] // ec
    tok = lambda dt: pl.BlockSpec((tt, d), lambda t, c: (t, 0))
    row_scores = pl.BlockSpec((PEER_HEADS, rows_per_chunk, tt), lambda t, c: (0, c, t))
    col_scores = pl.BlockSpec((PEER_HEADS, PEER_N_KEYS, tt), lambda t, c: (0, 0, t))
    return pl.pallas_call(
        functools.partial(_peer_kernel, rows_per_chunk=rows_per_chunk),
        grid=(n // tt, n_chunks),
        in_specs=[tok(BF16), tok(F32),
                  pl.BlockSpec((ec, d), lambda t, c: (c, 0)),
                  pl.BlockSpec((d, ec), lambda t, c: (0, c)),
                  row_scores, row_scores, col_scores, col_scores,
                  pl.BlockSpec(g.shape, lambda t, c: (0, 0))],
        out_specs=pl.BlockSpec((tt, d), lambda t, c: (t, 0)),
        out_shape=jax.ShapeDtypeStruct((n, d), F32),
        scratch_shapes=[pltpu.VMEM((d, tt), F32), pltpu.VMEM((ec, tt), F32), pltpu.VMEM((ec, tt), BF16)],
        compiler_params=pltpu.CompilerParams(
            dimension_semantics=("arbitrary", "arbitrary"), vmem_limit_bytes=VMEM_LIMIT),
        name="peer",
    )(hn, x1, u, vt, s1, e1, s2, e2, g)


def _pad_heads(w, head_dim):
    k = w.shape[0]
    w = w.reshape(k, -1, head_dim)
    return jnp.pad(w, ((0, 0), (0, 0), (0, LANES - head_dim))).reshape(k, -1)


def _rot_cols(w, head_dim):
    k = w.shape[0]
    w = w.reshape(k, -1, head_dim)
    return jnp.concatenate([-w[..., head_dim // 2:], w[..., :head_dim // 2]], axis=-1).reshape(k, -1)


def _rope_angles(seq, dim):
    inv = 1.0 / (ROPE_THETA ** (jnp.arange(0, dim, 2, dtype=F32) / dim))
    ang = jnp.arange(seq, dtype=F32)[:, None] * inv[None, :]
    return jnp.cos(ang), jnp.sin(ang)


def _layer_weights(w_in, w_uq, w_uk, w_uv, w_o):
    qa, ka, va, cq, ckv, kr = jnp.split(
        w_in, (SWA_WIDTH, 2 * SWA_WIDTH, 3 * SWA_WIDTH, 3 * SWA_WIDTH + MLA_Q_RANK,
               3 * SWA_WIDTH + MLA_Q_RANK + MLA_KV_RANK), axis=1)
    qa = qa * (SWA_HEAD_DIM ** -0.5)
    hd = SWA_HEAD_DIM
    w1 = jnp.concatenate(
        [_pad_heads(qa, hd), _pad_heads(_rot_cols(qa, hd), hd),
         _pad_heads(ka, hd), _pad_heads(_rot_cols(ka, hd), hd), _pad_heads(va, hd),
         cq, ckv, kr, _rot_cols(kr, MLA_ROPE_DIM),
         jnp.zeros((D_MODEL, LANES - 2 * MLA_ROPE_DIM), F32)], axis=1).astype(BF16)

    qk_dim = MLA_NOPE_DIM + MLA_ROPE_DIM
    uq = (w_uq * (qk_dim ** -0.5)).reshape(MLA_Q_RANK, MLA_HEADS, qk_dim)
    uq_rope = uq[..., MLA_NOPE_DIM:]
    uq_rot = jnp.concatenate([jnp.zeros_like(uq[..., :MLA_NOPE_DIM]),
                              -uq_rope[..., MLA_ROPE_DIM // 2:], uq_rope[..., :MLA_ROPE_DIM // 2]], axis=-1)
    wq = jnp.concatenate([_pad_heads(uq.reshape(MLA_Q_RANK, -1), qk_dim),
                          _pad_heads(uq_rot.reshape(MLA_Q_RANK, -1), qk_dim)], axis=1).astype(BF16)
    wkv = jnp.concatenate([_pad_heads(w_uk, MLA_NOPE_DIM), _pad_heads(w_uv, MLA_V_DIM)], axis=1).astype(BF16)

    place = np.zeros((LANES, SLAB_W), np.float32)
    for h in range(MLA_HEADS):
        for c in range(MLA_ROPE_DIM):
            place[c, h * LANES + MLA_NOPE_DIM + c] = 1.0
            place[MLA_ROPE_DIM + c, h * LANES + MLA_NOPE_DIM + c] = 1.0
    wo = jnp.pad(w_o.reshape(2 * N_HEADS, SWA_HEAD_DIM, D_MODEL),
                 ((0, 0), (0, LANES - SWA_HEAD_DIM), (0, 0))).reshape(2 * SLAB_W, D_MODEL).astype(BF16)
    return w1, wq, wkv, jnp.asarray(place, BF16), wo


def _rope_tables(seq):
    cos_a, sin_a = _rope_angles(seq, SWA_HEAD_DIM)
    cos_b, sin_b = _rope_angles(seq, MLA_ROPE_DIM)
    ta_c = jnp.tile(cos_a, (1, LANES // cos_a.shape[1]))
    ta_s = jnp.tile(sin_a, (1, LANES // sin_a.shape[1]))
    ones = jnp.ones((seq, MLA_NOPE_DIM), F32)
    zeros = lambda w: jnp.zeros((seq, w), F32)
    pad = LANES - MLA_NOPE_DIM - MLA_ROPE_DIM
    tq_c = jnp.concatenate([ones, cos_b, cos_b, zeros(pad)], axis=1)
    tq_s = jnp.concatenate([zeros(MLA_NOPE_DIM), sin_b, sin_b, zeros(pad)], axis=1)
    tk = jnp.concatenate([cos_b, cos_b, sin_b, sin_b, zeros(LANES - 2 * MLA_ROPE_DIM)], axis=1)
    return ta_c, ta_s, tq_c, tq_s, tk


def kernel(x, attn_norm_g, w_in, mla_q_norm_g, mla_kv_norm_g, w_uq, w_uk, w_uv, w_o, ffn_norm_g,
           peer_w_query, peer_sub_keys, peer_u, peer_v, final_norm_g):
    b, s, d = x.shape
    depth = w_in.shape[0]
    assert depth == 1, "the final rms_norm is fused into the last stage of a single layer"
    tabs = _rope_tables(s)
    row = lambda v: v.reshape(1, -1)
    for layer in range(depth):
        w1, wq, wkv, place, wo = _layer_weights(w_in[layer], w_uq[layer], w_uk[layer], w_uv[layer],
                                                w_o[layer])
        qa, ka, va, qb, kb, vb = _proj_call(
            x, row(attn_norm_g[layer]), w1, row(mla_q_norm_g[layer]), row(mla_kv_norm_g[layer]),
            wq, wkv, place, tabs, tm=256)
        oa = _dilated_call(qa, ka, va)
        ob = _mla_call(qb, kb, vb, tq=256, tk=512)
        keys = peer_sub_keys[layer].reshape(2 * PEER_HEADS, PEER_N_KEYS, PEER_HALF_DIM).astype(BF16)
        x1, hn, s1, e1, s2, e2 = _mix_call(x, oa, ob, wo, row(ffn_norm_g[layer]),
                                           peer_w_query[layer].astype(BF16), keys, tm=256)
        out = _peer_call(hn.reshape(b * s, d), x1.reshape(b * s, d), peer_u[layer].astype(BF16),
                         peer_v[layer].T.astype(BF16), s1, e1, s2, e2, row(final_norm_g),
                         tt=512, rows_per_chunk=8)
        x = out.reshape(b, s, d)
    return x
```

```python
import functools

import jax
import jax.numpy as jnp
import numpy as np
from jax import lax
from jax.experimental import pallas as pl
from jax.experimental.pallas import tpu as pltpu

D_MODEL = 1024
SWA_HEADS = 8
SWA_HEAD_DIM = 64
DILATIONS = (1, 4, 16)
SPAN = 128
MLA_HEADS = 8
MLA_NOPE_DIM = 64
MLA_ROPE_DIM = 32
MLA_V_DIM = 64
MLA_Q_RANK = 256
MLA_KV_RANK = 256
ROPE_THETA = 10000.0
SWA_WIDTH = SWA_HEADS * SWA_HEAD_DIM
PEER_HEADS = 8
PEER_N_KEYS = 128
PEER_TOPK = 16
PEER_HALF_DIM = 128
NORM_EPS = 1e-6
NEG_INF = -1e30

LANES = 128
N_HEADS = 8
SLAB_W = N_HEADS * LANES
ONES_LANE = SWA_HEAD_DIM
VMEM_LIMIT = 56 * 1024 * 1024

F32 = jnp.float32
BF16 = jnp.bfloat16


def _rms(x, g):
    return x * lax.rsqrt(jnp.mean(x * x, axis=-1, keepdims=True) + NORM_EPS) * g


def _dot(a, b):
    return jnp.dot(a, b, preferred_element_type=F32)


def _dot_nt(a, b):
    return lax.dot_general(a, b, (((1,), (1,)), ((), ())), preferred_element_type=F32)


_C_QA, _C_QAR, _C_KA, _C_KAR, _C_VA = 0, SLAB_W, 2 * SLAB_W, 3 * SLAB_W, 4 * SLAB_W
_C_CQ = 5 * SLAB_W
_C_CKV = _C_CQ + MLA_Q_RANK
_C_KR = _C_CKV + MLA_KV_RANK
_C_END = _C_KR + LANES


def _proj_kernel(x_ref, g_ref, w1_ref, gq_ref, gkv_ref, wq_ref, wkv_ref, place_ref,
                 ta_c_ref, ta_s_ref, tq_c_ref, tq_s_ref, tk_ref,
                 qa_ref, ka_ref, va_ref, qb_ref, kb_ref, vb_ref):
    hn = _rms(x_ref[0], g_ref[...]).astype(BF16)
    half = SLAB_W // 2

    def mm(lo, width):
        return _dot(hn, w1_ref[:, lo:lo + width])

    def store_heads(out_ref, h0, val):
        for j in range(val.shape[1] // LANES):
            out_ref[h0 + j, 0] = val[:, j * LANES:(j + 1) * LANES].astype(BF16)

    def rope_heads(out_ref, h0, plain, rot, c, s):
        for j in range(plain.shape[1] // LANES):
            sl = slice(j * LANES, (j + 1) * LANES)
            out_ref[h0 + j, 0] = (plain[:, sl] * c + rot[:, sl] * s).astype(BF16)

    ca, sa = ta_c_ref[...], ta_s_ref[...]
    lane = lax.broadcasted_iota(jnp.int32, (1, half), 1)
    ones_lane = jnp.where(lane % LANES == ONES_LANE, 1.0, 0.0)
    for part in range(2):
        off = part * half
        rope_heads(qa_ref, part * 4, mm(_C_QA + off, half), mm(_C_QAR + off, half), ca, sa)
        rope_heads(ka_ref, part * 4, mm(_C_KA + off, half), mm(_C_KAR + off, half), ca, sa)
        store_heads(va_ref, part * 4, mm(_C_VA + off, half) + ones_lane)

    cqn = _rms(mm(_C_CQ, MLA_Q_RANK), gq_ref[...]).astype(BF16)
    cq_c, cq_s = tq_c_ref[...], tq_s_ref[...]
    for part in range(2):
        off = part * half
        rope_heads(qb_ref, part * 4, _dot(cqn, wq_ref[:, off:off + half]),
                   _dot(cqn, wq_ref[:, SLAB_W + off:SLAB_W + off + half]), cq_c, cq_s)

    ckvn = _rms(mm(_C_CKV, MLA_KV_RANK), gkv_ref[...]).astype(BF16)
    kr = (mm(_C_KR, LANES) * tk_ref[...]).astype(BF16)
    for part in range(2):
        off = part * half
        store_heads(kb_ref, part * 4,
                    _dot(ckvn, wkv_ref[:, off:off + half]) + _dot(kr, place_ref[:, off:off + half]))
        store_heads(vb_ref, part * 4, _dot(ckvn, wkv_ref[:, SLAB_W + off:SLAB_W + off + half]))


def _proj_call(x, g, w1, gq, gkv, wq, wkv, place, tabs, tm):
    b, s, d = x.shape
    const = lambda shape: pl.BlockSpec(shape, lambda bi, i: (0,) * len(shape))
    tab = pl.BlockSpec((tm, LANES), lambda bi, i: (i, 0))
    head_out = pl.BlockSpec((N_HEADS, 1, tm, LANES), lambda bi, i: (0, bi, i, 0))
    out_sds = jax.ShapeDtypeStruct((N_HEADS, b, s, LANES), BF16)
    return pl.pallas_call(
        _proj_kernel,
        grid=(b, s // tm),
        in_specs=[pl.BlockSpec((1, tm, d), lambda bi, i: (bi, i, 0)),
                  const(g.shape), const(w1.shape), const(gq.shape), const(gkv.shape),
                  const(wq.shape), const(wkv.shape), const(place.shape),
                  tab, tab, tab, tab, tab],
        out_specs=[head_out] * 6,
        out_shape=[out_sds] * 6,
        compiler_params=pltpu.CompilerParams(
            dimension_semantics=("arbitrary", "arbitrary"), vmem_limit_bytes=VMEM_LIMIT),
        name="proj",
    )(x, g, w1, gq, gkv, wq, wkv, place, *tabs)


DILATED_GROUP = 4


def _dilated_kernel(q1, k1, v1, q4, k4, v4, q16, k16, v16, o_ref, m_s, acc_s):
    views = {1: (q1, k1, v1), 4: (q4, k4, v4), 16: (q16, k16, v16)}
    seq = o_ref.shape[2]
    row = lax.broadcasted_iota(jnp.int32, (SPAN, 2 * SPAN), 0)
    col = lax.broadcasted_iota(jnp.int32, (SPAN, 2 * SPAN), 1)
    upper = jnp.where(col <= row + SPAN, 0.0, NEG_INF)
    band_bias = jnp.where(col >= row, upper, NEG_INF)
    first_bias = jnp.where(col >= SPAN, upper, NEG_INF)

    def compute(d, r, n):
        q_ref, k_ref, v_ref = views[d]
        lanes = slice(r * LANES, (r + 1) * LANES)
        static = isinstance(n, int)
        prev = max(n - 1, 0) if static else jnp.maximum(n - 1, 0)

        def rows_of(ref, j):
            start = j * SPAN if static else pl.multiple_of(j * SPAN, SPAN)
            return ref[0, 0, pl.ds(start, SPAN), lanes]

        if static:
            bias = band_bias if n > 0 else first_bias
        else:
            bias = jnp.where(n > 0, band_bias, first_bias)
        k = jnp.concatenate([rows_of(k_ref, prev), rows_of(k_ref, n)], axis=0)
        v = jnp.concatenate([rows_of(v_ref, prev), rows_of(v_ref, n)], axis=0)
        s = _dot_nt(rows_of(q_ref, n), k) + bias
        m = jnp.max(s, axis=1, keepdims=True)
        p = jnp.exp(s - m)
        return m, _dot(p.astype(BF16), v)

    def merge(d, r, n, m, pv):
        m = jnp.broadcast_to(m, (SPAN, LANES))
        if d == 1:
            start = n * SPAN if isinstance(n, int) else pl.multiple_of(n * SPAN, SPAN)
            rows = pl.ds(start, SPAN)
            m_s[rows, :] = m
            acc_s[rows, :] = pv
        else:
            rows = pl.ds(n * SPAN * d + r, SPAN, stride=d)
            m_old = m_s[rows, :]
            m_new = jnp.maximum(m_old, m)
            m_s[rows, :] = m_new
            acc_s[rows, :] = acc_s[rows, :] * jnp.exp(m_old - m_new) + pv * jnp.exp(m - m_new)

    def run(d, r, blocks):
        results = [compute(d, r, n) for n in blocks]
        for n, (m, pv) in zip(blocks, results):
            merge(d, r, n, m, pv)

    for d in DILATIONS:
        n_blocks = seq // d // SPAN
        group = min(DILATED_GROUP, n_blocks)
        for r in range(d):
            if n_blocks == group:
                run(d, r, list(range(n_blocks)))
            else:
                def body(it, carry, d=d, r=r, group=group):
                    run(d, r, [it * group + j for j in range(group)])
                    return carry

                lax.fori_loop(0, n_blocks // group, body, 0)

    acc = acc_s[...]
    o_ref[0, 0] = (acc / acc[:, ONES_LANE:ONES_LANE + 1]).astype(BF16)


def _dilated_call(qa, ka, va):
    h, b, s, _ = qa.shape
    operands, specs = [], []
    for d in DILATIONS:
        for t in (qa, ka, va):
            operands.append(t.reshape(h, b, s // d, d * LANES))
            specs.append(pl.BlockSpec((1, 1, s // d, d * LANES), lambda hi, bi: (hi, bi, 0, 0)))
    return pl.pallas_call(
        _dilated_kernel,
        grid=(h, b),
        in_specs=specs,
        out_specs=pl.BlockSpec((1, 1, s, LANES), lambda hi, bi: (hi, bi, 0, 0)),
        out_shape=jax.ShapeDtypeStruct((h, b, s, LANES), BF16),
        scratch_shapes=[pltpu.VMEM((s, LANES), F32)] * 2,
        compiler_params=pltpu.CompilerParams(
            dimension_semantics=("arbitrary", "arbitrary"), vmem_limit_bytes=VMEM_LIMIT),
        name="dilated",
    )(*operands)


def _mla_kernel(q_ref, k_ref, v_ref, o_ref, *, tq, tk):
    qi = pl.program_id(2)
    q = q_ref[0, 0]
    row = lax.broadcasted_iota(jnp.int32, (tq, tk), 0)
    col = lax.broadcasted_iota(jnp.int32, (tq, tk), 1)

    def step(j, carry, masked):
        m, l, acc = carry
        rows = pl.ds(pl.multiple_of(j * tk, tk), tk)
        s = _dot_nt(q, k_ref[0, 0, rows, :])
        if masked:
            s = jnp.where(col + j * tk <= row + qi * tq, s, NEG_INF)
        m_new = jnp.maximum(m, jnp.max(s, axis=1, keepdims=True))
        a = jnp.exp(m - m_new)
        p = jnp.exp(s - m_new)
        l = l * a + jnp.sum(p, axis=1, keepdims=True)
        acc = acc * a + _dot(p.astype(BF16), v_ref[0, 0, rows, :])
        return m_new, l, acc

    init = (jnp.full((tq, 1), NEG_INF, F32), jnp.zeros((tq, 1), F32), jnp.zeros((tq, LANES), F32))
    n_full = (qi * tq) // tk
    n_all = ((qi + 1) * tq + tk - 1) // tk
    carry = lax.fori_loop(0, n_full, functools.partial(step, masked=False), init)
    m, l, acc = lax.fori_loop(n_full, n_all, functools.partial(step, masked=True), carry)
    o_ref[0, 0] = (acc / l).astype(BF16)


def _mla_call(qb, kb, vb, tq, tk):
    h, b, s, _ = qb.shape
    kv_spec = pl.BlockSpec((1, 1, s, LANES), lambda hi, bi, qi: (hi, bi, 0, 0))
    q_spec = pl.BlockSpec((1, 1, tq, LANES), lambda hi, bi, qi: (hi, bi, qi, 0))
    return pl.pallas_call(
        functools.partial(_mla_kernel, tq=tq, tk=tk),
        grid=(h, b, s // tq),
        in_specs=[q_spec, kv_spec, kv_spec],
        out_specs=q_spec,
        out_shape=jax.ShapeDtypeStruct((h, b, s, LANES), BF16),
        compiler_params=pltpu.CompilerParams(
            dimension_semantics=("arbitrary", "arbitrary", "arbitrary"), vmem_limit_bytes=VMEM_LIMIT),
        name="mla",
    )(qb, kb, vb)


SUBLANES = 8
_TOP_ROWS = 24


def _top_values(x, k, rows):
    vals = []
    for _ in range(k):
        m = jnp.max(x, axis=0, keepdims=True)
        vals.append(m)
        x = jnp.where(x == m, NEG_INF, x)
    vals.append(jnp.full((rows - k, x.shape[1]), NEG_INF, F32))
    return jnp.concatenate(vals, axis=0)


def _kth_and_next(x, k):
    for _ in range(k - 1):
        m = jnp.max(x, axis=0, keepdims=True)
        x = jnp.where(x == m, NEG_INF, x)
    kth = jnp.max(x, axis=0, keepdims=True)
    x = jnp.where(x == kth, NEG_INF, x)
    return kth, jnp.max(x, axis=0, keepdims=True)


def _mix_kernel(x_ref, oa_ref, ob_ref, wo_ref, g_ref, wq_ref, keys_ref,
                x1_ref, hn_ref, rho_ref, e1_ref, s2_ref, e2_ref):
    mixed = jnp.concatenate([oa_ref[h, 0] for h in range(N_HEADS)]
                            + [ob_ref[h, 0] for h in range(N_HEADS)], axis=1)
    x1 = x_ref[0] + _dot(mixed, wo_ref[...])
    x1_ref[0] = x1
    hn = _rms(x1, g_ref[...]).astype(BF16)
    hn_ref[0] = hn
    query = _dot(hn, wq_ref[...]).astype(BF16)
    k1 = PEER_TOPK + 1
    for h in range(PEER_HEADS):
        lo = 2 * h * PEER_HALF_DIM
        s1 = _dot_nt(keys_ref[2 * h], query[:, lo:lo + PEER_HALF_DIM])
        s2 = _dot_nt(keys_ref[2 * h + 1], query[:, lo + PEER_HALF_DIM:lo + 2 * PEER_HALF_DIM])
        top1 = _top_values(s1, k1, _TOP_ROWS)
        top2 = _top_values(s2, k1, _TOP_ROWS)
        groups = [(top1, top2[0:1])]
        groups += [(top1[0:SUBLANES], top2[l:l + 1]) for l in range(1, SUBLANES)]
        groups += [(top1[0:1], top2[SUBLANES:_TOP_ROWS])]
        cand = jnp.concatenate([a + b for a, b in groups], axis=0)
        c16, c17 = _kth_and_next(cand, PEER_TOPK)
        tau = 0.5 * (c16 + c17)
        best = top1[0:1] + top2[0:1]
        z = jnp.zeros_like(tau)
        for a, b in groups:
            z = z + jnp.sum(jnp.where(b >= tau - a, jnp.exp(a + b - best), 0.0), axis=0, keepdims=True)
        rho_ref[h] = tau - s1
        e1_ref[h] = jnp.exp(s1 - top1[0:1]) / z
        e2 = jnp.exp(s2 - top2[0:1])
        for t in range(s2.shape[1] // LANES):
            s2_ref[h, t] = s2[:, t * LANES:(t + 1) * LANES]
            e2_ref[h, t] = e2[:, t * LANES:(t + 1) * LANES]


def _mix_call(x, oa, ob, wo, g, wq, keys, tm):
    b, s, d = x.shape
    const = lambda shape: pl.BlockSpec(shape, lambda bi, i: (0,) * len(shape))
    head_in = pl.BlockSpec((N_HEADS, 1, tm, LANES), lambda bi, i: (0, bi, i, 0))
    tok = pl.BlockSpec((1, tm, d), lambda bi, i: (bi, i, 0))
    tiles = s // tm
    row = pl.BlockSpec((PEER_HEADS, PEER_N_KEYS, tm), lambda bi, i: (0, 0, bi * tiles + i))
    row_sds = jax.ShapeDtypeStruct((PEER_HEADS, PEER_N_KEYS, b * s), F32)
    col = pl.BlockSpec((PEER_HEADS, tm // LANES, PEER_N_KEYS, LANES), lambda bi, i: (0, bi * tiles + i, 0, 0))
    col_sds = jax.ShapeDtypeStruct((PEER_HEADS, b * s // LANES, PEER_N_KEYS, LANES), F32)
    return pl.pallas_call(
        _mix_kernel,
        grid=(b, tiles),
        in_specs=[tok, head_in, head_in, const(wo.shape), const(g.shape), const(wq.shape),
                  const(keys.shape)],
        out_specs=[tok, tok, row, row, col, col],
        out_shape=[jax.ShapeDtypeStruct((b, s, d), F32), jax.ShapeDtypeStruct((b, s, d), BF16),
                   row_sds, row_sds, col_sds, col_sds],
        compiler_params=pltpu.CompilerParams(
            dimension_semantics=("arbitrary", "arbitrary"), vmem_limit_bytes=VMEM_LIMIT),
        name="mix",
    )(x, oa, ob, wo, g, wq, keys)


def _gelu(x):
    return 0.5 * x * (1.0 + lax.erf(x * 0.7071067811865476))


def _peer_kernel(hn_ref, x1_ref, u_ref, vt_ref, rho_ref, e1_ref, s2_ref, e2_ref, g_ref,
                 out_ref, acc_s, hid_s, act_s, *, rows_per_chunk):
    c = pl.program_id(1)
    n_slabs = hid_s.shape[0]

    @pl.when(c == 0)
    def _():
        acc_s[...] = jnp.zeros_like(acc_s)

    hid = _dot_nt(u_ref[...], hn_ref[...])
    for t in range(n_slabs):
        hid_s[t] = hid[:, t * LANES:(t + 1) * LANES]

    def slab_body(t, carry):
        lanes = pl.ds(pl.multiple_of(t * LANES, LANES), LANES)
        for i in range(rows_per_chunk):
            rows = slice(i * PEER_N_KEYS, (i + 1) * PEER_N_KEYS)
            gate = jnp.zeros((PEER_N_KEYS, LANES), F32)
            for h in range(PEER_HEADS):
                weight = e2_ref[h, t] * e1_ref[h, i:i + 1, lanes]
                gate = gate + jnp.where(s2_ref[h, t] >= rho_ref[h, i:i + 1, lanes], weight, 0.0)
            act_s[t, rows, :] = (_gelu(hid_s[t, rows, :]) * gate).astype(BF16)
        return carry

    lax.fori_loop(0, n_slabs, slab_body, 0)
    act = jnp.concatenate([act_s[t] for t in range(n_slabs)], axis=1)
    acc_s[...] += _dot(vt_ref[...], act)

    @pl.when(c == pl.num_programs(1) - 1)
    def _():
        y = x1_ref[...] + acc_s[...].T
        out_ref[...] = _rms(y, g_ref[...])


def _peer_call(hn, x1, u, vt, rho, e1, s2, e2, g, tt, rows_per_chunk):
    n, d = hn.shape
    ec = rows_per_chunk * PEER_N_KEYS
    n_chunks = u.shape[0] // ec
    n_slabs = tt // LANES
    tok = pl.BlockSpec((tt, d), lambda t, c: (t, 0))
    row_scores = pl.BlockSpec((PEER_HEADS, rows_per_chunk, tt), lambda t, c: (0, c, t))
    col_scores = pl.BlockSpec((PEER_HEADS, n_slabs, PEER_N_KEYS, LANES), lambda t, c: (0, t, 0, 0))
    return pl.pallas_call(
        functools.partial(_peer_kernel, rows_per_chunk=rows_per_chunk),
        grid=(n // tt, n_chunks),
        in_specs=[tok, tok,
                  pl.BlockSpec((ec, d), lambda t, c: (c, 0)),
                  pl.BlockSpec((d, ec), lambda t, c: (0, c)),
                  row_scores, row_scores, col_scores, col_scores,
                  pl.BlockSpec(g.shape, lambda t, c: (0, 0))],
        out_specs=pl.BlockSpec((tt, d), lambda t, c: (t, 0)),
        out_shape=jax.ShapeDtypeStruct((n, d), F32),
        scratch_shapes=[pltpu.VMEM((d, tt), F32), pltpu.VMEM((n_slabs, ec, LANES), F32),
                        pltpu.VMEM((n_slabs, ec, LANES), BF16)],
        compiler_params=pltpu.CompilerParams(
            dimension_semantics=("arbitrary", "arbitrary"), vmem_limit_bytes=VMEM_LIMIT),
        name="peer",
    )(hn, x1, u, vt, rho, e1, s2, e2, g)


def _pad_heads(w, head_dim):
    k = w.shape[0]
    w = w.reshape(k, -1, head_dim)
    return jnp.pad(w, ((0, 0), (0, 0), (0, LANES - head_dim))).reshape(k, -1)


def _rot_cols(w, head_dim):
    k = w.shape[0]
    w = w.reshape(k, -1, head_dim)
    return jnp.concatenate([-w[..., head_dim // 2:], w[..., :head_dim // 2]], axis=-1).reshape(k, -1)


def _rope_angles(seq, dim):
    inv = 1.0 / (ROPE_THETA ** (jnp.arange(0, dim, 2, dtype=F32) / dim))
    ang = jnp.arange(seq, dtype=F32)[:, None] * inv[None, :]
    return jnp.cos(ang), jnp.sin(ang)


def _layer_weights(w_in, w_uq, w_uk, w_uv, w_o):
    qa, ka, va, cq, ckv, kr = jnp.split(
        w_in, (SWA_WIDTH, 2 * SWA_WIDTH, 3 * SWA_WIDTH, 3 * SWA_WIDTH + MLA_Q_RANK,
               3 * SWA_WIDTH + MLA_Q_RANK + MLA_KV_RANK), axis=1)
    qa = qa * (SWA_HEAD_DIM ** -0.5)
    hd = SWA_HEAD_DIM
    w1 = jnp.concatenate(
        [_pad_heads(qa, hd), _pad_heads(_rot_cols(qa, hd), hd),
         _pad_heads(ka, hd), _pad_heads(_rot_cols(ka, hd), hd), _pad_heads(va, hd),
         cq, ckv, kr, _rot_cols(kr, MLA_ROPE_DIM),
         jnp.zeros((D_MODEL, LANES - 2 * MLA_ROPE_DIM), F32)], axis=1).astype(BF16)

    qk_dim = MLA_NOPE_DIM + MLA_ROPE_DIM
    uq = (w_uq * (qk_dim ** -0.5)).reshape(MLA_Q_RANK, MLA_HEADS, qk_dim)
    uq_rope = uq[..., MLA_NOPE_DIM:]
    uq_rot = jnp.concatenate([jnp.zeros_like(uq[..., :MLA_NOPE_DIM]),
                              -uq_rope[..., MLA_ROPE_DIM // 2:], uq_rope[..., :MLA_ROPE_DIM // 2]], axis=-1)
    wq = jnp.concatenate([_pad_heads(uq.reshape(MLA_Q_RANK, -1), qk_dim),
                          _pad_heads(uq_rot.reshape(MLA_Q_RANK, -1), qk_dim)], axis=1).astype(BF16)
    wkv = jnp.concatenate([_pad_heads(w_uk, MLA_NOPE_DIM), _pad_heads(w_uv, MLA_V_DIM)], axis=1).astype(BF16)

    place = np.zeros((LANES, SLAB_W), np.float32)
    for h in range(MLA_HEADS):
        for c in range(MLA_ROPE_DIM):
            place[c, h * LANES + MLA_NOPE_DIM + c] = 1.0
            place[MLA_ROPE_DIM + c, h * LANES + MLA_NOPE_DIM + c] = 1.0
    wo = jnp.pad(w_o.reshape(2 * N_HEADS, SWA_HEAD_DIM, D_MODEL),
                 ((0, 0), (0, LANES - SWA_HEAD_DIM), (0, 0))).reshape(2 * SLAB_W, D_MODEL).astype(BF16)
    return w1, wq, wkv, jnp.asarray(place, BF16), wo


def _rope_tables(seq):
    cos_a, sin_a = _rope_angles(seq, SWA_HEAD_DIM)
    cos_b, sin_b = _rope_angles(seq, MLA_ROPE_DIM)
    ta_c = jnp.tile(cos_a, (1, LANES // cos_a.shape[1]))
    ta_s = jnp.tile(sin_a, (1, LANES // sin_a.shape[1]))
    ones = jnp.ones((seq, MLA_NOPE_DIM), F32)
    zeros = lambda w: jnp.zeros((seq, w), F32)
    pad = LANES - MLA_NOPE_DIM - MLA_ROPE_DIM
    tq_c = jnp.concatenate([ones, cos_b, cos_b, zeros(pad)], axis=1)
    tq_s = jnp.concatenate([zeros(MLA_NOPE_DIM), sin_b, sin_b, zeros(pad)], axis=1)
    tk = jnp.concatenate([cos_b, cos_b, sin_b, sin_b, zeros(LANES - 2 * MLA_ROPE_DIM)], axis=1)
    return ta_c, ta_s, tq_c, tq_s, tk


def kernel(x, attn_norm_g, w_in, mla_q_norm_g, mla_kv_norm_g, w_uq, w_uk, w_uv, w_o, ffn_norm_g,
           peer_w_query, peer_sub_keys, peer_u, peer_v, final_norm_g):
    b, s, d = x.shape
    depth = w_in.shape[0]
    assert depth == 1, "the final rms_norm is fused into the last stage of a single layer"
    tabs = _rope_tables(s)
    row = lambda v: v.reshape(1, -1)
    for layer in range(depth):
        w1, wq, wkv, place, wo = _layer_weights(w_in[layer], w_uq[layer], w_uk[layer], w_uv[layer],
                                                w_o[layer])
        qa, ka, va, qb, kb, vb = _proj_call(
            x, row(attn_norm_g[layer]), w1, row(mla_q_norm_g[layer]), row(mla_kv_norm_g[layer]),
            wq, wkv, place, tabs, tm=256)
        oa = _dilated_call(qa, ka, va)
        ob = _mla_call(qb, kb, vb, tq=512, tk=512)
        keys = peer_sub_keys[layer].reshape(2 * PEER_HEADS, PEER_N_KEYS, PEER_HALF_DIM).astype(BF16)
        x1, hn, rho, e1, s2, e2 = _mix_call(x, oa, ob, wo, row(ffn_norm_g[layer]),
                                            peer_w_query[layer].astype(BF16), keys, tm=256)
        out = _peer_call(hn.reshape(b * s, d), x1.reshape(b * s, d), peer_u[layer].astype(BF16),
                         peer_v[layer].T.astype(BF16), rho, e1, s2, e2, row(final_norm_g),
                         tt=512, rows_per_chunk=8)
        x = out.reshape(b, s, d)
    return x
```

```python
import functools

import jax
import jax.numpy as jnp
import numpy as np
from jax import lax
from jax.experimental import pallas as pl
from jax.experimental.pallas import tpu as pltpu

D_MODEL = 1024
SWA_HEADS = 8
SWA_HEAD_DIM = 64
DILATIONS = (1, 4, 16)
SPAN = 128
MLA_HEADS = 8
MLA_NOPE_DIM = 64
MLA_ROPE_DIM = 32
MLA_V_DIM = 64
MLA_Q_RANK = 256
MLA_KV_RANK = 256
ROPE_THETA = 10000.0
SWA_WIDTH = SWA_HEADS * SWA_HEAD_DIM
PEER_HEADS = 8
PEER_N_KEYS = 128
PEER_TOPK = 16
PEER_HALF_DIM = 128
NORM_EPS = 1e-6
NEG_INF = -1e30

LANES = 128
N_HEADS = 8
SLAB_W = N_HEADS * LANES
ONES_LANE = SWA_HEAD_DIM
VMEM_LIMIT = 56 * 1024 * 1024

F32 = jnp.float32
BF16 = jnp.bfloat16


def _rms(x, g):
    return x * lax.rsqrt(jnp.mean(x * x, axis=-1, keepdims=True) + NORM_EPS) * g


def _dot(a, b):
    return jnp.dot(a, b, preferred_element_type=F32)


def _dot_nt(a, b):
    return lax.dot_general(a, b, (((1,), (1,)), ((), ())), preferred_element_type=F32)


_C_QA, _C_QAR, _C_KA, _C_KAR, _C_VA = 0, SLAB_W, 2 * SLAB_W, 3 * SLAB_W, 4 * SLAB_W
_C_CQ = 5 * SLAB_W
_C_CKV = _C_CQ + MLA_Q_RANK
_C_KR = _C_CKV + MLA_KV_RANK
_C_END = _C_KR + LANES


def _proj_kernel(x_ref, g_ref, w1_ref, gq_ref, gkv_ref, wq_ref, wkv_ref, place_ref,
                 ta_c_ref, ta_s_ref, tq_c_ref, tq_s_ref, tk_ref,
                 qa1, qa4, qa16, ka1, ka4, ka16, va1, va4, va16, qb_ref, kb_ref, vb_ref, stage_ref):
    hn = _rms(x_ref[0], g_ref[...]).astype(BF16)
    half = SLAB_W // 2
    tm = x_ref.shape[1]

    def mm(lo, width):
        return _dot(hn, w1_ref[:, lo:lo + width])

    def store_heads(out_ref, h0, val):
        for j in range(val.shape[1] // LANES):
            out_ref[h0 + j, 0] = val[:, j * LANES:(j + 1) * LANES].astype(BF16)

    def store_views(views, h0, slabs):
        for j, slab in enumerate(slabs):
            h = h0 + j
            views[0][h, 0] = slab.astype(BF16)
            stage_ref[h] = slab
            for d, view in zip(DILATIONS[1:], views[1:]):
                for r in range(d):
                    view[h, 0, :, r * LANES:(r + 1) * LANES] = (
                        stage_ref[h, pl.ds(r, tm // d, stride=d), :].astype(BF16))

    def slabs_of(val):
        return [val[:, j * LANES:(j + 1) * LANES] for j in range(val.shape[1] // LANES)]

    def rope(plain, rot, c, s):
        return [p * c + r * s for p, r in zip(slabs_of(plain), slabs_of(rot))]

    def rope_heads(out_ref, h0, plain, rot, c, s):
        for j, slab in enumerate(rope(plain, rot, c, s)):
            out_ref[h0 + j, 0] = slab.astype(BF16)

    ca, sa = ta_c_ref[...], ta_s_ref[...]
    lane = lax.broadcasted_iota(jnp.int32, (1, LANES), 1)
    ones_lane = jnp.where(lane == ONES_LANE, 1.0, 0.0)
    for part in range(2):
        off = part * half
        store_views((qa1, qa4, qa16), part * 4, rope(mm(_C_QA + off, half), mm(_C_QAR + off, half), ca, sa))
        store_views((ka1, ka4, ka16), part * 4, rope(mm(_C_KA + off, half), mm(_C_KAR + off, half), ca, sa))
        store_views((va1, va4, va16), part * 4, [v + ones_lane for v in slabs_of(mm(_C_VA + off, half))])

    cqn = _rms(mm(_C_CQ, MLA_Q_RANK), gq_ref[...]).astype(BF16)
    cq_c, cq_s = tq_c_ref[...], tq_s_ref[...]
    for part in range(2):
        off = part * half
        rope_heads(qb_ref, part * 4, _dot(cqn, wq_ref[:, off:off + half]),
                   _dot(cqn, wq_ref[:, SLAB_W + off:SLAB_W + off + half]), cq_c, cq_s)

    ckvn = _rms(mm(_C_CKV, MLA_KV_RANK), gkv_ref[...]).astype(BF16)
    kr = (mm(_C_KR, LANES) * tk_ref[...]).astype(BF16)
    for part in range(2):
        off = part * half
        store_heads(kb_ref, part * 4,
                    _dot(ckvn, wkv_ref[:, off:off + half]) + _dot(kr, place_ref[:, off:off + half]))
        vb = _dot(ckvn, wkv_ref[:, SLAB_W + off:SLAB_W + off + half])
        for j, slab in enumerate(slabs_of(vb)):
            vb_ref[part * 4 + j, 0] = (slab + ones_lane).astype(BF16)


def _proj_call(x, g, w1, gq, gkv, wq, wkv, place, tabs, tm):
    b, s, d = x.shape
    const = lambda shape: pl.BlockSpec(shape, lambda bi, i: (0,) * len(shape))
    tab = pl.BlockSpec((tm, LANES), lambda bi, i: (i, 0))
    def view(dil):
        spec = pl.BlockSpec((N_HEADS, 1, tm // dil, dil * LANES), lambda bi, i: (0, bi, i, 0))
        return spec, jax.ShapeDtypeStruct((N_HEADS, b, s // dil, dil * LANES), BF16)

    views = [view(dil) for _ in range(3) for dil in DILATIONS]
    outs = views + [view(1)] * 3
    return pl.pallas_call(
        _proj_kernel,
        grid=(b, s // tm),
        in_specs=[pl.BlockSpec((1, tm, d), lambda bi, i: (bi, i, 0)),
                  const(g.shape), const(w1.shape), const(gq.shape), const(gkv.shape),
                  const(wq.shape), const(wkv.shape), const(place.shape),
                  tab, tab, tab, tab, tab],
        out_specs=[spec for spec, _ in outs],
        out_shape=[sds for _, sds in outs],
        scratch_shapes=[pltpu.VMEM((N_HEADS, tm, LANES), F32)],
        compiler_params=pltpu.CompilerParams(
            dimension_semantics=("arbitrary", "arbitrary"), vmem_limit_bytes=VMEM_LIMIT),
        name="proj",
    )(x, g, w1, gq, gkv, wq, wkv, place, *tabs)


DILATED_GROUP = 4


def _dilated_kernel(q1, k1, v1, q4, k4, v4, q16, k16, v16, o_ref, m_s, acc_s):
    views = {1: (q1, k1, v1), 4: (q4, k4, v4), 16: (q16, k16, v16)}
    seq = o_ref.shape[2]
    row = lax.broadcasted_iota(jnp.int32, (SPAN, 2 * SPAN), 0)
    col = lax.broadcasted_iota(jnp.int32, (SPAN, 2 * SPAN), 1)
    upper = jnp.where(col <= row + SPAN, 0.0, NEG_INF)
    band_bias = jnp.where(col >= row, upper, NEG_INF)
    first_bias = jnp.where(col >= SPAN, upper, NEG_INF)

    def compute(d, r, n):
        q_ref, k_ref, v_ref = views[d]
        lanes = slice(r * LANES, (r + 1) * LANES)
        static = isinstance(n, int)
        prev = max(n - 1, 0) if static else jnp.maximum(n - 1, 0)

        def rows_of(ref, j):
            start = j * SPAN if static else pl.multiple_of(j * SPAN, SPAN)
            return ref[0, 0, pl.ds(start, SPAN), lanes]

        if static:
            bias = band_bias if n > 0 else first_bias
        else:
            bias = jnp.where(n > 0, band_bias, first_bias)
        k = jnp.concatenate([rows_of(k_ref, prev), rows_of(k_ref, n)], axis=0)
        v = jnp.concatenate([rows_of(v_ref, prev), rows_of(v_ref, n)], axis=0)
        s = _dot_nt(rows_of(q_ref, n), k) + bias
        m = jnp.max(s, axis=1, keepdims=True)
        p = jnp.exp(s - m)
        return m, _dot(p.astype(BF16), v)

    def merge(d, r, n, m, pv):
        m = jnp.broadcast_to(m, (SPAN, LANES))
        if d == 1:
            start = n * SPAN if isinstance(n, int) else pl.multiple_of(n * SPAN, SPAN)
            rows = pl.ds(start, SPAN)
            m_s[rows, :] = m
            acc_s[rows, :] = pv
        else:
            rows = pl.ds(n * SPAN * d + r, SPAN, stride=d)
            m_old = m_s[rows, :]
            m_new = jnp.maximum(m_old, m)
            m_s[rows, :] = m_new
            acc_s[rows, :] = acc_s[rows, :] * jnp.exp(m_old - m_new) + pv * jnp.exp(m - m_new)

    def run(d, r, blocks):
        results = [compute(d, r, n) for n in blocks]
        for n, (m, pv) in zip(blocks, results):
            merge(d, r, n, m, pv)

    for d in DILATIONS:
        n_blocks = seq // d // SPAN
        group = min(DILATED_GROUP, n_blocks)
        for r in range(d):
            if n_blocks == group:
                run(d, r, list(range(n_blocks)))
            else:
                def body(it, carry, d=d, r=r, group=group):
                    run(d, r, [it * group + j for j in range(group)])
                    return carry

                lax.fori_loop(0, n_blocks // group, body, 0)

    acc = acc_s[...]
    o_ref[0, 0] = (acc / acc[:, ONES_LANE:ONES_LANE + 1]).astype(BF16)


def _dilated_call(q_views, k_views, v_views):
    h, b, s, _ = q_views[0].shape
    operands, specs = [], []
    for i, d in enumerate(DILATIONS):
        for views in (q_views, k_views, v_views):
            operands.append(views[i])
            specs.append(pl.BlockSpec((1, 1, s // d, d * LANES), lambda hi, bi: (hi, bi, 0, 0)))
    return pl.pallas_call(
        _dilated_kernel,
        grid=(h, b),
        in_specs=specs,
        out_specs=pl.BlockSpec((1, 1, s, LANES), lambda hi, bi: (hi, bi, 0, 0)),
        out_shape=jax.ShapeDtypeStruct((h, b, s, LANES), BF16),
        scratch_shapes=[pltpu.VMEM((s, LANES), F32)] * 2,
        compiler_params=pltpu.CompilerParams(
            dimension_semantics=("arbitrary", "arbitrary"), vmem_limit_bytes=VMEM_LIMIT),
        name="dilated",
    )(*operands)


MLA_SUB = 128


def _mla_kernel(q_ref, k_ref, v_ref, o_ref, m_s, acc_s, *, tq, tk):
    qi = pl.program_id(2)
    n_heads = q_ref.shape[0]
    row = lax.broadcasted_iota(jnp.int32, (MLA_SUB, LANES), 0)
    col = lax.broadcasted_iota(jnp.int32, (MLA_SUB, LANES), 1)
    m_s[...] = jnp.full(m_s.shape, NEG_INF, F32)
    acc_s[...] = jnp.zeros_like(acc_s)

    def step(j, carry, masked):
        rows = pl.ds(pl.multiple_of(j * tk, tk), tk)
        scores = [_dot_nt(q_ref[g, 0], k_ref[g, 0, rows, :]) for g in range(n_heads)]
        probs, scales = [], []
        for g in range(n_heads):
            p_rows, a_rows = [], []
            for u in range(tq // MLA_SUB):
                sub = slice(u * MLA_SUB, (u + 1) * MLA_SUB)
                slabs = [scores[g][sub, c * LANES:(c + 1) * LANES] for c in range(tk // LANES)]
                if masked:
                    first_row = qi * tq + u * MLA_SUB
                    slabs = [jnp.where(col + (j * tk + c * LANES) <= row + first_row, x, NEG_INF)
                             for c, x in enumerate(slabs)]
                m_blk = slabs[0]
                for x in slabs[1:]:
                    m_blk = jnp.maximum(m_blk, x)
                m_old = m_s[g, sub, :]
                m_new = jnp.maximum(m_old, jnp.max(m_blk, axis=1, keepdims=True))
                m_s[g, sub, :] = m_new
                p_rows.append(jnp.concatenate([jnp.exp2(x - m_new) for x in slabs], axis=1).astype(BF16))
                a_rows.append(jnp.exp2(m_old - m_new))
            probs.append(jnp.concatenate(p_rows, axis=0))
            scales.append(jnp.concatenate(a_rows, axis=0))
        for g in range(n_heads):
            acc_s[g] = acc_s[g] * scales[g] + _dot(probs[g], v_ref[g, 0, rows, :])
        return carry

    n_full = (qi * tq) // tk
    n_all = ((qi + 1) * tq + tk - 1) // tk
    lax.fori_loop(0, n_full, functools.partial(step, masked=False), 0)
    lax.fori_loop(n_full, n_all, functools.partial(step, masked=True), 0)
    for g in range(n_heads):
        acc = acc_s[g]
        o_ref[g, 0] = (acc / acc[:, ONES_LANE:ONES_LANE + 1]).astype(BF16)


MLA_HEADS_PER_STEP = 2


def _mla_call(qb, kb, vb, tq, tk):
    h, b, s, _ = qb.shape
    hg = MLA_HEADS_PER_STEP
    kv_spec = pl.BlockSpec((hg, 1, s, LANES), lambda hi, bi, qi: (hi, bi, 0, 0))
    q_spec = pl.BlockSpec((hg, 1, tq, LANES), lambda hi, bi, qi: (hi, bi, qi, 0))
    return pl.pallas_call(
        functools.partial(_mla_kernel, tq=tq, tk=tk),
        grid=(h // hg, b, s // tq),
        in_specs=[q_spec, kv_spec, kv_spec],
        out_specs=q_spec,
        out_shape=jax.ShapeDtypeStruct((h, b, s, LANES), BF16),
        scratch_shapes=[pltpu.VMEM((hg, tq, LANES), F32)] * 2,
        compiler_params=pltpu.CompilerParams(
            dimension_semantics=("arbitrary", "arbitrary", "arbitrary"), vmem_limit_bytes=VMEM_LIMIT),
        name="mla",
    )(qb, kb, vb)


SUBLANES = 8
_TOP_ROWS = 24


def _top_values(x, k, rows):
    vals = []
    for _ in range(k):
        m = jnp.max(x, axis=0, keepdims=True)
        vals.append(m)
        x = jnp.where(x == m, NEG_INF, x)
    vals.append(jnp.full((rows - k, x.shape[1]), NEG_INF, F32))
    return jnp.concatenate(vals, axis=0)


def _kth_and_next(x, k):
    for _ in range(k - 1):
        m = jnp.max(x, axis=0, keepdims=True)
        x = jnp.where(x == m, NEG_INF, x)
    kth = jnp.max(x, axis=0, keepdims=True)
    x = jnp.where(x == kth, NEG_INF, x)
    return kth, jnp.max(x, axis=0, keepdims=True)


def _mix_kernel(x_ref, oa_ref, ob_ref, wo_ref, g_ref, wq_ref, keys_ref,
                x1_ref, hn_ref, rho_ref, e1_ref, s2_ref, e2_ref):
    mixed = jnp.concatenate([oa_ref[h, 0] for h in range(N_HEADS)]
                            + [ob_ref[h, 0] for h in range(N_HEADS)], axis=1)
    x1 = x_ref[0] + _dot(mixed, wo_ref[...])
    x1_ref[0] = x1
    hn_f32 = _rms(x1, g_ref[...])
    hn = hn_f32.astype(BF16)
    hn_ref[...] = hn_f32.T.astype(BF16)
    query = _dot(hn, wq_ref[...]).astype(BF16)
    k1 = PEER_TOPK + 1
    for h in range(PEER_HEADS):
        lo = 2 * h * PEER_HALF_DIM
        s1 = _dot_nt(keys_ref[2 * h], query[:, lo:lo + PEER_HALF_DIM])
        s2 = _dot_nt(keys_ref[2 * h + 1], query[:, lo + PEER_HALF_DIM:lo + 2 * PEER_HALF_DIM])
        top1 = _top_values(s1, k1, _TOP_ROWS)
        top2 = _top_values(s2, k1, _TOP_ROWS)
        groups = [(top1, top2[0:1])]
        groups += [(top1[0:SUBLANES], top2[l:l + 1]) for l in range(1, SUBLANES)]
        groups += [(top1[0:1], top2[SUBLANES:_TOP_ROWS])]
        cand = jnp.concatenate([a + b for a, b in groups], axis=0)
        c16, c17 = _kth_and_next(cand, PEER_TOPK)
        tau = 0.5 * (c16 + c17)
        best = top1[0:1] + top2[0:1]
        z = jnp.zeros_like(tau)
        for a, b in groups:
            z = z + jnp.sum(jnp.where(b >= tau - a, jnp.exp(a + b - best), 0.0), axis=0, keepdims=True)
        rho_ref[h] = tau - s1
        e1_ref[h] = jnp.exp(s1 - top1[0:1]) / z
        e2 = jnp.exp(s2 - top2[0:1])
        for t in range(s2.shape[1] // LANES):
            s2_ref[h, t] = s2[:, t * LANES:(t + 1) * LANES]
            e2_ref[h, t] = e2[:, t * LANES:(t + 1) * LANES]


def _mix_call(x, oa, ob, wo, g, wq, keys, tm):
    b, s, d = x.shape
    const = lambda shape: pl.BlockSpec(shape, lambda bi, i: (0,) * len(shape))
    head_in = pl.BlockSpec((N_HEADS, 1, tm, LANES), lambda bi, i: (0, bi, i, 0))
    tok = pl.BlockSpec((1, tm, d), lambda bi, i: (bi, i, 0))
    tiles = s // tm
    row = pl.BlockSpec((PEER_HEADS, PEER_N_KEYS, tm), lambda bi, i: (0, 0, bi * tiles + i))
    row_sds = jax.ShapeDtypeStruct((PEER_HEADS, PEER_N_KEYS, b * s), F32)
    col = pl.BlockSpec((PEER_HEADS, tm // LANES, PEER_N_KEYS, LANES), lambda bi, i: (0, bi * tiles + i, 0, 0))
    col_sds = jax.ShapeDtypeStruct((PEER_HEADS, b * s // LANES, PEER_N_KEYS, LANES), F32)
    return pl.pallas_call(
        _mix_kernel,
        grid=(b, tiles),
        in_specs=[tok, head_in, head_in, const(wo.shape), const(g.shape), const(wq.shape),
                  const(keys.shape)],
        out_specs=[tok, pl.BlockSpec((d, tm), lambda bi, i: (0, bi * tiles + i)), row, row, col, col],
        out_shape=[jax.ShapeDtypeStruct((b, s, d), F32), jax.ShapeDtypeStruct((d, b * s), BF16),
                   row_sds, row_sds, col_sds, col_sds],
        compiler_params=pltpu.CompilerParams(
            dimension_semantics=("arbitrary", "arbitrary"), vmem_limit_bytes=VMEM_LIMIT),
        name="mix",
    )(x, oa, ob, wo, g, wq, keys)


def _gelu(x):
    return 0.5 * x * (1.0 + lax.erf(x * 0.7071067811865476))


def _peer_kernel(hn_ref, x1_ref, u_ref, vt_ref, rho_ref, e1_ref, s2_ref, e2_ref, g_ref,
                 out_ref, acc_s, hid_a, hid_b, act_a, act_b, *, rows_per_chunk, n_chunks):
    g = pl.program_id(0)
    c2 = jnp.maximum(g - 2, 0) % n_chunks
    n_slabs = hid_a.shape[0]

    @pl.when(g == 0)
    def _():
        hid_b[...] = jnp.zeros_like(hid_b)
        act_a[...] = jnp.zeros_like(act_a)

    @pl.when(c2 == 0)
    def _():
        acc_s[...] = jnp.zeros_like(acc_s)

    def stages(hid_w, hid_r, act_w, act_r):
        piece = 2 * PEER_N_KEYS
        total = None
        for p in range(rows_per_chunk * PEER_N_KEYS // piece):
            span = slice(p * piece, (p + 1) * piece)
            hid = _dot(u_ref[span, :], hn_ref[...])
            for t in range(n_slabs):
                hid_w[t, span, :] = hid[:, t * LANES:(t + 1) * LANES]

            for i in range(p * piece // PEER_N_KEYS, (p + 1) * piece // PEER_N_KEYS):
                rows = slice(i * PEER_N_KEYS, (i + 1) * PEER_N_KEYS)
                for t in range(n_slabs):
                    lanes = slice(t * LANES, (t + 1) * LANES)
                    gate = jnp.zeros((PEER_N_KEYS, LANES), F32)
                    for h in range(PEER_HEADS):
                        weight = e2_ref[h, t] * e1_ref[h, i:i + 1, lanes]
                        gate = gate + jnp.where(s2_ref[h, t] >= rho_ref[h, i:i + 1, lanes], weight, 0.0)
                    act_w[t, rows, :] = (_gelu(hid_r[t, rows, :]) * gate).astype(BF16)

            act = jnp.concatenate([act_r[t, span, :] for t in range(n_slabs)], axis=1)
            part = _dot(vt_ref[:, span], act)
            total = part if total is None else total + part
        acc_s[...] += total

    @pl.when(g % 2 == 0)
    def _():
        stages(hid_a, hid_b, act_b, act_a)

    @pl.when(g % 2 == 1)
    def _():
        stages(hid_b, hid_a, act_a, act_b)

    @pl.when((g >= 2) & (c2 == n_chunks - 1))
    def _():
        y = x1_ref[...] + acc_s[...].T
        out_ref[...] = _rms(y, g_ref[...])


def _peer_call(hn, x1, u, vt, rho, e1, s2, e2, g, tt, rows_per_chunk):
    d, n = hn.shape
    ec = rows_per_chunk * PEER_N_KEYS
    n_chunks = u.shape[0] // ec
    n_slabs = tt // LANES
    last = (n // tt) * n_chunks - 1

    def item(lag):
        def split(step):
            i = jnp.clip(step - lag, 0, last)
            return i // n_chunks, i % n_chunks
        return split

    now, gate_stage, out_stage = item(0), item(1), item(2)
    row_scores = pl.BlockSpec((PEER_HEADS, rows_per_chunk, tt),
                              lambda s_: (0, gate_stage(s_)[1], gate_stage(s_)[0]))
    col_scores = pl.BlockSpec((PEER_HEADS, n_slabs, PEER_N_KEYS, LANES),
                              lambda s_: (0, gate_stage(s_)[0], 0, 0))
    return pl.pallas_call(
        functools.partial(_peer_kernel, rows_per_chunk=rows_per_chunk, n_chunks=n_chunks),
        grid=(last + 3,),
        in_specs=[pl.BlockSpec((d, tt), lambda s_: (0, now(s_)[0])),
                  pl.BlockSpec((tt, d), lambda s_: (out_stage(s_)[0], 0)),
                  pl.BlockSpec((ec, d), lambda s_: (now(s_)[1], 0)),
                  pl.BlockSpec((d, ec), lambda s_: (0, out_stage(s_)[1])),
                  row_scores, row_scores, col_scores, col_scores,
                  pl.BlockSpec(g.shape, lambda s_: (0, 0))],
        out_specs=pl.BlockSpec((tt, d), lambda s_: (out_stage(s_)[0], 0)),
        out_shape=jax.ShapeDtypeStruct((n, d), F32),
        scratch_shapes=[pltpu.VMEM((d, tt), F32),
                        pltpu.VMEM((n_slabs, ec, LANES), F32), pltpu.VMEM((n_slabs, ec, LANES), F32),
                        pltpu.VMEM((n_slabs, ec, LANES), BF16), pltpu.VMEM((n_slabs, ec, LANES), BF16)],
        compiler_params=pltpu.CompilerParams(
            dimension_semantics=("arbitrary",), vmem_limit_bytes=VMEM_LIMIT),
        name="peer",
    )(hn, x1, u, vt, rho, e1, s2, e2, g)


def _pad_heads(w, head_dim):
    k = w.shape[0]
    w = w.reshape(k, -1, head_dim)
    return jnp.pad(w, ((0, 0), (0, 0), (0, LANES - head_dim))).reshape(k, -1)


def _rot_cols(w, head_dim):
    k = w.shape[0]
    w = w.reshape(k, -1, head_dim)
    return jnp.concatenate([-w[..., head_dim // 2:], w[..., :head_dim // 2]], axis=-1).reshape(k, -1)


def _rope_angles(seq, dim):
    inv = 1.0 / (ROPE_THETA ** (jnp.arange(0, dim, 2, dtype=F32) / dim))
    ang = jnp.arange(seq, dtype=F32)[:, None] * inv[None, :]
    return jnp.cos(ang), jnp.sin(ang)


def _layer_weights(w_in, w_uq, w_uk, w_uv, w_o):
    qa, ka, va, cq, ckv, kr = jnp.split(
        w_in, (SWA_WIDTH, 2 * SWA_WIDTH, 3 * SWA_WIDTH, 3 * SWA_WIDTH + MLA_Q_RANK,
               3 * SWA_WIDTH + MLA_Q_RANK + MLA_KV_RANK), axis=1)
    qa = qa * (SWA_HEAD_DIM ** -0.5)
    hd = SWA_HEAD_DIM
    w1 = jnp.concatenate(
        [_pad_heads(qa, hd), _pad_heads(_rot_cols(qa, hd), hd),
         _pad_heads(ka, hd), _pad_heads(_rot_cols(ka, hd), hd), _pad_heads(va, hd),
         cq, ckv, kr, _rot_cols(kr, MLA_ROPE_DIM),
         jnp.zeros((D_MODEL, LANES - 2 * MLA_ROPE_DIM), F32)], axis=1).astype(BF16)

    qk_dim = MLA_NOPE_DIM + MLA_ROPE_DIM
    uq = (w_uq * (qk_dim ** -0.5 * np.log2(np.e))).reshape(MLA_Q_RANK, MLA_HEADS, qk_dim)
    uq_rope = uq[..., MLA_NOPE_DIM:]
    uq_rot = jnp.concatenate([jnp.zeros_like(uq[..., :MLA_NOPE_DIM]),
                              -uq_rope[..., MLA_ROPE_DIM // 2:], uq_rope[..., :MLA_ROPE_DIM // 2]], axis=-1)
    wq = jnp.concatenate([_pad_heads(uq.reshape(MLA_Q_RANK, -1), qk_dim),
                          _pad_heads(uq_rot.reshape(MLA_Q_RANK, -1), qk_dim)], axis=1).astype(BF16)
    wkv = jnp.concatenate([_pad_heads(w_uk, MLA_NOPE_DIM), _pad_heads(w_uv, MLA_V_DIM)], axis=1).astype(BF16)

    place = np.zeros((LANES, SLAB_W), np.float32)
    for h in range(MLA_HEADS):
        for c in range(MLA_ROPE_DIM):
            place[c, h * LANES + MLA_NOPE_DIM + c] = 1.0
            place[MLA_ROPE_DIM + c, h * LANES + MLA_NOPE_DIM + c] = 1.0
    wo = jnp.pad(w_o.reshape(2 * N_HEADS, SWA_HEAD_DIM, D_MODEL),
                 ((0, 0), (0, LANES - SWA_HEAD_DIM), (0, 0))).reshape(2 * SLAB_W, D_MODEL).astype(BF16)
    return w1, wq, wkv, jnp.asarray(place, BF16), wo


def _rope_tables(seq):
    cos_a, sin_a = _rope_angles(seq, SWA_HEAD_DIM)
    cos_b, sin_b = _rope_angles(seq, MLA_ROPE_DIM)
    ta_c = jnp.tile(cos_a, (1, LANES // cos_a.shape[1]))
    ta_s = jnp.tile(sin_a, (1, LANES // sin_a.shape[1]))
    ones = jnp.ones((seq, MLA_NOPE_DIM), F32)
    zeros = lambda w: jnp.zeros((seq, w), F32)
    pad = LANES - MLA_NOPE_DIM - MLA_ROPE_DIM
    tq_c = jnp.concatenate([ones, cos_b, cos_b, zeros(pad)], axis=1)
    tq_s = jnp.concatenate([zeros(MLA_NOPE_DIM), sin_b, sin_b, zeros(pad)], axis=1)
    tk = jnp.concatenate([cos_b, cos_b, sin_b, sin_b, zeros(LANES - 2 * MLA_ROPE_DIM)], axis=1)
    return ta_c, ta_s, tq_c, tq_s, tk


def kernel(x, attn_norm_g, w_in, mla_q_norm_g, mla_kv_norm_g, w_uq, w_uk, w_uv, w_o, ffn_norm_g,
           peer_w_query, peer_sub_keys, peer_u, peer_v, final_norm_g):
    b, s, d = x.shape
    depth = w_in.shape[0]
    assert depth == 1, "the final rms_norm is fused into the last stage of a single layer"
    tabs = _rope_tables(s)
    row = lambda v: v.reshape(1, -1)
    for layer in range(depth):
        w1, wq, wkv, place, wo = _layer_weights(w_in[layer], w_uq[layer], w_uk[layer], w_uv[layer],
                                                w_o[layer])
        outs = _proj_call(
            x, row(attn_norm_g[layer]), w1, row(mla_q_norm_g[layer]), row(mla_kv_norm_g[layer]),
            wq, wkv, place, tabs, tm=256)
        qb, kb, vb = outs[9:]
        oa = _dilated_call(outs[0:3], outs[3:6], outs[6:9])
        ob = _mla_call(qb, kb, vb, tq=512, tk=512)
        keys = peer_sub_keys[layer].reshape(2 * PEER_HEADS, PEER_N_KEYS, PEER_HALF_DIM).astype(BF16)
        x1, hn, rho, e1, s2, e2 = _mix_call(x, oa, ob, wo, row(ffn_norm_g[layer]),
                                            peer_w_query[layer].astype(BF16), keys, tm=256)
        out = _peer_call(hn, x1.reshape(b * s, d), peer_u[layer].astype(BF16),
                         peer_v[layer].T.astype(BF16), rho, e1, s2, e2, row(final_norm_g),
                         tt=512, rows_per_chunk=8)
        x = out.reshape(b, s, d)
    return x
```
